```python
import math
import jax, jax.numpy as jnp
from jax import lax
import numpy as np

D_MODEL = 2048
BATCH = 16
SEQ = 2048
DEPTH = 4

CHUNK = 64
N_MEM = 256
D_MIX = D_MODEL
MLSTM_WIDTH = D_MIX // 4
MLSTM_HEADS = 4
MLSTM_HEAD_DIM = MLSTM_WIDTH // MLSTM_HEADS
MLSTM_CONV_TAPS = 4
CCONV_WIDTH = D_MIX // 4
CCONV_TAPS = 31
ATTN_WIDTH = D_MIX - MLSTM_WIDTH - CCONV_WIDTH
ATTN_HEAD_DIM = 128
ATTN_HEADS = ATTN_WIDTH // ATTN_HEAD_DIM
LEFT_CHUNKS = 8
BAND = (LEFT_CHUNKS + 1) * CHUNK
REL_CLIP = 256
X_HEADS = 4
X_HEAD_DIM = D_MODEL // X_HEADS
D_FF = 256 * ((8 * D_MODEL // 3 + 255) // 256)
FFN_CONV_TAPS = 3
LN_EPS = 1e-5
DN_ALPHA = (2 * DEPTH) ** 0.25
DN_BETA = (8 * DEPTH) ** -0.25
IN_SIZES = (MLSTM_WIDTH, MLSTM_WIDTH, MLSTM_WIDTH, MLSTM_WIDTH, MLSTM_HEADS, MLSTM_HEADS,
            CCONV_WIDTH, CCONV_WIDTH, ATTN_WIDTH, ATTN_WIDTH, ATTN_WIDTH)
N_IN = sum(IN_SIZES)

kernel_name = 'hybrid_mlstm_conformer_chunkattn_deepnorm'


def _layer_norm(x, g, b):
    xf = x.astype(jnp.float32)
    mu = jnp.mean(xf, axis=-1, keepdims=True)
    var = jnp.mean(jnp.square(xf - mu), axis=-1, keepdims=True)
    y = (xf - mu) * lax.rsqrt(var + LN_EPS)
    return (y * g.astype(jnp.float32) + b.astype(jnp.float32)).astype(x.dtype)


def _head_norm(h, g):
    mu = jnp.mean(h, axis=-1, keepdims=True)
    var = jnp.mean(jnp.square(h - mu), axis=-1, keepdims=True)
    return (h - mu) * lax.rsqrt(var + LN_EPS) * g.astype(jnp.float32).reshape(h.shape[-2], h.shape[-1])


def _causal_dwconv(x, w, b):
    k = w.shape[0]
    y = lax.conv_general_dilated(x, w[:, None, :].astype(x.dtype), window_strides=(1,),
                                 padding=[(k - 1, 0)], dimension_numbers=('NWC', 'WIO', 'NWC'),
                                 feature_group_count=x.shape[-1])
    return y + b.astype(x.dtype)


def _mlstm(q, k, v, i_pre, f_pre):
    B, S, H, Dh = q.shape
    NC = S // CHUNK

    def to_chunks(t):
        return t.reshape(B, NC, CHUNK, H, Dh).transpose(0, 3, 1, 2, 4)

    q, k, v = to_chunks(q), to_chunks(k) * (Dh ** -0.5), to_chunks(v)
    ig = i_pre.reshape(B, NC, CHUNK, H).transpose(0, 3, 1, 2)
    lf = jax.nn.log_sigmoid(f_pre).reshape(B, NC, CHUNK, H).transpose(0, 3, 1, 2)
    b = jnp.cumsum(lf, axis=-1)
    g = b[..., -1]
    a = g[..., None] - b + ig
    a_max = jnp.max(a, axis=-1)
    w_a = jnp.exp(a - a_max[..., None])
    kv = jnp.einsum('bhcl,bhcld,bhcle->bhcde', w_a, k, v)
    nk = jnp.einsum('bhcl,bhcld->bhcd', w_a, k)

    def step(carry, xs):
        C, n, m = carry
        g_c, amax_c, kv_c, nk_c = xs
        m_new = jnp.maximum(g_c + m, amax_c)
        dec = jnp.exp(g_c + m - m_new)
        wc = jnp.exp(amax_c - m_new)
        C_new = dec[..., None, None] * C + wc[..., None, None] * kv_c
        n_new = dec[..., None] * n + wc[..., None] * nk_c
        return (C_new, n_new, m_new), (C, n, m)

    init = (jnp.zeros((B, H, Dh, Dh), jnp.float32), jnp.zeros((B, H, Dh), jnp.float32),
            jnp.zeros((B, H), jnp.float32))
    xs = (jnp.moveaxis(g, 2, 0), jnp.moveaxis(a_max, 2, 0), jnp.moveaxis(kv, 2, 0), jnp.moveaxis(nk, 2, 0))
    _, (C_prev, n_prev, m_prev) = lax.scan(step, init, xs)
    C_prev = jnp.moveaxis(C_prev, 0, 2)
    n_prev = jnp.moveaxis(n_prev, 0, 2)
    m_prev = jnp.moveaxis(m_prev, 0, 2)

    causal = jnp.tril(jnp.ones((CHUNK, CHUNK), dtype=bool))
    d_log = jnp.where(causal, b[..., :, None] - b[..., None, :] + ig[..., None, :], -jnp.inf)
    m_inter = b + m_prev[..., None]
    m_t = jnp.maximum(jnp.max(d_log, axis=-1), m_inter)
    s = jnp.einsum('bhcld,bhcsd->bhcls', q, k) * jnp.exp(d_log - m_t[..., None])
    w_inter = jnp.exp(m_inter - m_t)
    num = jnp.einsum('bhcls,bhcse->bhcle', s, v) + w_inter[..., None] * jnp.einsum('bhcld,bhcde->bhcle', q, C_prev)
    den = jnp.sum(s, axis=-1) + w_inter * jnp.einsum('bhcld,bhcd->bhcl', q, n_prev)
    h = num / jnp.maximum(jnp.abs(den), jnp.exp(-m_t))[..., None]
    return h.transpose(0, 2, 3, 1, 4).reshape(B, S, H, Dh)


def _chunk_attention(q, k, v, rel_bias):
    B, S, H, Dh = q.shape
    NC = S // CHUNK
    pad = LEFT_CHUNKS * CHUNK
    k_pad = jnp.pad(k, ((0, 0), (pad, 0), (0, 0), (0, 0)))
    v_pad = jnp.pad(v, ((0, 0), (pad, 0), (0, 0), (0, 0)))
    lq = jnp.arange(CHUNK)
    lk = jnp.arange(BAND)
    dist = (pad + lq[:, None]) - lk[None, :]
    bias = rel_bias[:, jnp.clip(dist, -REL_CLIP, REL_CLIP) + REL_CLIP].astype(jnp.float32)
    q_chunks = q.reshape(B, NC, CHUNK, H, Dh).transpose(1, 0, 2, 3, 4)
    scale = Dh ** -0.5

    def one_chunk(args):
        c, qc = args
        start = c * CHUNK
        kb = lax.dynamic_slice_in_dim(k_pad, start, BAND, axis=1)
        vb = lax.dynamic_slice_in_dim(v_pad, start, BAND, axis=1)
        s = jnp.einsum('blhd,bkhd->bhlk', qc, kb).astype(jnp.float32) * scale + bias
        valid = (start - pad + lk) >= 0
        s = jnp.where(valid, s, -jnp.inf)
        p = jax.nn.softmax(s, axis=-1).astype(vb.dtype)
        return jnp.einsum('bhlk,bkhd->blhd', p, vb)

    out = lax.map(one_chunk, (jnp.arange(NC), q_chunks))
    return out.transpose(1, 0, 2, 3, 4).reshape(B, S, H * Dh)


def _hybrid_mixer(x, w_in, mlstm_conv_w, mlstm_conv_b, mlstm_ig_b, mlstm_fg_b, mlstm_norm_g,
                  cconv_w, cconv_b, cconv_ln_g, cconv_ln_b, rel_bias, w_out):
    B, S, _ = x.shape
    f32 = jnp.float32
    offsets = []
    acc = 0
    for size in IN_SIZES[:-1]:
        acc += size
        offsets.append(acc)
    proj = x @ w_in
    mq, mk, mv, mo, mi, mf, ca, cg, aq, ak, av = jnp.split(proj, offsets, axis=-1)

    qk = jax.nn.silu(_causal_dwconv(jnp.concatenate([mq, mk], axis=-1), mlstm_conv_w, mlstm_conv_b))
    mq, mk = jnp.split(qk, 2, axis=-1)
    mh = lambda t: t.reshape(B, S, MLSTM_HEADS, MLSTM_HEAD_DIM).astype(f32)
    h_t = _mlstm(mh(mq), mh(mk), mh(mv), (mi + mlstm_ig_b).astype(f32), (mf + mlstm_fg_b).astype(f32))
    h_m = jax.nn.sigmoid(mh(mo)) * _head_norm(h_t, mlstm_norm_g)
    h_m = h_m.reshape(B, S, MLSTM_WIDTH).astype(x.dtype)

    c = ca * jax.nn.sigmoid(cg)
    c = _causal_dwconv(c, cconv_w, cconv_b)
    h_c = jax.nn.silu(_layer_norm(c, cconv_ln_g, cconv_ln_b))

    ah = lambda t: t.reshape(B, S, ATTN_HEADS, ATTN_HEAD_DIM)
    h_a = _chunk_attention(ah(aq), ah(ak), ah(av), rel_bias)

    return jnp.concatenate([h_m, h_c, h_a], axis=-1) @ w_out


def _memory_cross_attention(x, mem, xq, xkv, xo):
    B, S, _ = x.shape
    M = mem.shape[1]
    q = (x @ xq).reshape(B, S, X_HEADS, X_HEAD_DIM)
    k, v = jnp.split(mem @ xkv, 2, axis=-1)
    k = k.reshape(B, M, X_HEADS, X_HEAD_DIM)
    v = v.reshape(B, M, X_HEADS, X_HEAD_DIM)
    s = jnp.einsum('bshd,bmhd->bhsm', q, k).astype(jnp.float32) * (X_HEAD_DIM ** -0.5)
    p = jax.nn.softmax(s, axis=-1).astype(v.dtype)
    o = jnp.einsum('bhsm,bmhd->bshd', p, v).reshape(B, S, D_MODEL)
    return o @ xo


def _conv_ffn(x, w_up, conv_w, conv_b, w_down):
    gate, val = jnp.split(x @ w_up, 2, axis=-1)
    gate = _causal_dwconv(gate, conv_w, conv_b)
    return (jax.nn.gelu(gate, approximate=False) * val) @ w_down


def setup_inputs(seed: int = 0) -> dict:
    key = jax.random.key(seed)
    ks = jax.random.split(key, 32)
    f32 = jnp.float32
    L = DEPTH

    def nrm(k, shape, scale):
        return scale * jax.random.normal(k, shape, f32)

    return {
        'x': nrm(ks[0], (BATCH, SEQ, D_MODEL), 1.0),
        'mem': nrm(ks[1], (BATCH, N_MEM, D_MODEL), 1.0),
        'ln_in_g': 1.0 + nrm(ks[2], (D_MODEL,), 0.02),
        'ln_in_b': nrm(ks[3], (D_MODEL,), 0.02),
        'w_in': nrm(ks[4], (L, D_MODEL, N_IN), D_MODEL ** -0.5),
        'mlstm_conv_w': nrm(ks[5], (L, MLSTM_CONV_TAPS, 2 * MLSTM_WIDTH), MLSTM_CONV_TAPS ** -0.5),
        'mlstm_conv_b': nrm(ks[6], (L, 2 * MLSTM_WIDTH), 0.02),
        'mlstm_ig_b': nrm(ks[7], (L, MLSTM_HEADS), 0.1),
        'mlstm_fg_b': jnp.linspace(3.0, 6.0, MLSTM_HEADS, dtype=f32)[None, :] + nrm(ks[8], (L, MLSTM_HEADS), 0.1),
        'mlstm_norm_g': 1.0 + nrm(ks[9], (L, MLSTM_WIDTH), 0.02),
        'cconv_w': nrm(ks[10], (L, CCONV_TAPS, CCONV_WIDTH), CCONV_TAPS ** -0.5),
        'cconv_b': nrm(ks[11], (L, CCONV_WIDTH), 0.02),
        'cconv_ln_g': 1.0 + nrm(ks[12], (L, CCONV_WIDTH), 0.02),
        'cconv_ln_b': nrm(ks[13], (L, CCONV_WIDTH), 0.02),
        'rel_bias': nrm(ks[14], (L, ATTN_HEADS, 2 * REL_CLIP + 1), 0.2),
        'w_out': nrm(ks[15], (L, D_MIX, D_MODEL), DN_BETA * D_MIX ** -0.5),
        'ln1_g': 1.0 + nrm(ks[16], (L, D_MODEL), 0.02),
        'ln1_b': nrm(ks[17], (L, D_MODEL), 0.02),
        'xq': nrm(ks[18], (L, D_MODEL, D_MODEL), D_MODEL ** -0.5),
        'xkv': nrm(ks[19], (L, D_MODEL, 2 * D_MODEL), D_MODEL ** -0.5),
        'xo': nrm(ks[20], (L, D_MODEL, D_MODEL), DN_BETA * D_MODEL ** -0.5),
        'ln2_g': 1.0 + nrm(ks[21], (L, D_MODEL), 0.02),
        'ln2_b': nrm(ks[22], (L, D_MODEL), 0.02),
        'ffn_w_up': nrm(ks[23], (L, D_MODEL, 2 * D_FF), D_MODEL ** -0.5),
        'ffn_conv_w': nrm(ks[24], (L, FFN_CONV_TAPS, D_FF), FFN_CONV_TAPS ** -0.5),
        'ffn_conv_b': nrm(ks[25], (L, D_FF), 0.02),
        'ffn_w_down': nrm(ks[26], (L, D_FF, D_MODEL), DN_BETA * D_FF ** -0.5),
        'ln3_g': 1.0 + nrm(ks[27], (L, D_MODEL), 0.02),
        'ln3_b': nrm(ks[28], (L, D_MODEL), 0.02),
    }


def reference(x, mem, ln_in_g, ln_in_b, w_in, mlstm_conv_w, mlstm_conv_b, mlstm_ig_b, mlstm_fg_b,
              mlstm_norm_g, cconv_w, cconv_b, cconv_ln_g, cconv_ln_b, rel_bias, w_out, ln1_g, ln1_b,
              xq, xkv, xo, ln2_g, ln2_b, ffn_w_up, ffn_conv_w, ffn_conv_b, ffn_w_down, ln3_g, ln3_b):
    x = _layer_norm(x, ln_in_g, ln_in_b)
    for l in range(DEPTH):
        mix = _hybrid_mixer(x, w_in[l], mlstm_conv_w[l], mlstm_conv_b[l], mlstm_ig_b[l], mlstm_fg_b[l],
                            mlstm_norm_g[l], cconv_w[l], cconv_b[l], cconv_ln_g[l], cconv_ln_b[l],
                            rel_bias[l], w_out[l])
        x = _layer_norm(DN_ALPHA * x + mix, ln1_g[l], ln1_b[l])
        ca = _memory_cross_attention(x, mem, xq[l], xkv[l], xo[l])
        x = _layer_norm(DN_ALPHA * x + ca, ln2_g[l], ln2_b[l])
        ff = _conv_ffn(x, ffn_w_up[l], ffn_conv_w[l], ffn_conv_b[l], ffn_w_down[l])
        x = _layer_norm(DN_ALPHA * x + ff, ln3_g[l], ln3_b[l])
    return x
```

```python
import functools
import math

import jax
import jax.numpy as jnp
from jax import lax
from jax.experimental import pallas as pl
from jax.experimental.pallas import tpu as pltpu

F32 = jnp.float32
BF16 = jnp.bfloat16

DEPTH = 4
CHUNK = 64
LEFT_CHUNKS = 8
REL_CLIP = 256
MLSTM_HEADS = 4
HEAD_DIM = 128
MLSTM_WIDTH = MLSTM_HEADS * HEAD_DIM
MLSTM_CONV_TAPS = 4
CCONV_WIDTH = 512
CCONV_TAPS = 31
ATTN_HEADS = 8
ATTN_WIDTH = ATTN_HEADS * HEAD_DIM
X_HEADS = 4
FFN_CONV_TAPS = 3
LN_EPS = 1e-5
DN_ALPHA = (2 * DEPTH) ** 0.25

LANES = 128
SUBLANES = 8
VMEM_LIMIT_BYTES = 56 * 1024 * 1024

MLSTM_CHUNK = 128
ATTN_QBLOCK = 2 * CHUNK
ATTN_KTILES = (LEFT_CHUNKS * CHUNK + ATTN_QBLOCK) // ATTN_QBLOCK
ATTN_BIAS_TILES = ATTN_KTILES + LEFT_CHUNKS * CHUNK // ATTN_QBLOCK


def _params(*sem):
    return pltpu.CompilerParams(dimension_semantics=sem, vmem_limit_bytes=VMEM_LIMIT_BYTES)


def _sigmoid(x):
    return 1.0 / (1.0 + jnp.exp(-x))


def _log_sigmoid(x):
    return jnp.minimum(x, 0.0) - jnp.log1p(jnp.exp(-jnp.abs(x)))


def _layer_norm_rows(y, g, b):
    mu = jnp.mean(y, axis=-1, keepdims=True)
    d = y - mu
    var = jnp.mean(d * d, axis=-1, keepdims=True)
    return d * lax.rsqrt(var + LN_EPS) * g + b


def _ln_kernel(x_ref, g_ref, b_ref, of_ref, ob_ref):
    y = _layer_norm_rows(x_ref[...], g_ref[...], b_ref[...])
    of_ref[...] = y
    ob_ref[...] = y.astype(BF16)


def layer_norm(x, g, b, tm=512):
    m, d = x.shape
    return pl.pallas_call(
        _ln_kernel,
        grid=(m // tm,),
        in_specs=[pl.BlockSpec((tm, d), lambda i: (i, 0)),
                  pl.BlockSpec((1, d), lambda i: (0, 0)),
                  pl.BlockSpec((1, d), lambda i: (0, 0))],
        out_specs=[pl.BlockSpec((tm, d), lambda i: (i, 0)),
                   pl.BlockSpec((tm, d), lambda i: (i, 0))],
        out_shape=[jax.ShapeDtypeStruct((m, d), F32), jax.ShapeDtypeStruct((m, d), BF16)],
        compiler_params=_params("arbitrary"),
        name="ln_in",
    )(x, g.reshape(1, d), b.reshape(1, d))


def _mm_kernel(x_ref, w_ref, o_ref):
    o_ref[...] = jnp.dot(x_ref[...], w_ref[...], preferred_element_type=F32).astype(o_ref.dtype)


def matmul(x, w, out_dtype, tm=1024, tn=1024, name="mm"):
    m, k = x.shape
    n = w.shape[1]
    tm, tn = min(tm, m), min(tn, n)
    return pl.pallas_call(
        _mm_kernel,
        grid=(m // tm, n // tn),
        in_specs=[pl.BlockSpec((tm, k), lambda i, j: (i, 0)),
                  pl.BlockSpec((k, tn), lambda i, j: (0, j))],
        out_specs=pl.BlockSpec((tm, tn), lambda i, j: (i, j)),
        out_shape=jax.ShapeDtypeStruct((m, n), out_dtype),
        compiler_params=_params("arbitrary", "arbitrary"),
        name=name,
    )(x, w)


def _mm_ln_kernel(a_ref, w_ref, res_ref, g_ref, b_ref, of_ref, ob_ref, acc_ref, *, nk):
    part = jnp.dot(a_ref[...], w_ref[...], preferred_element_type=F32)

    def finish(branch):
        y = _layer_norm_rows(DN_ALPHA * res_ref[...] + branch, g_ref[...], b_ref[...])
        of_ref[...] = y
        ob_ref[...] = y.astype(BF16)

    if nk == 1:
        finish(part)
        return
    k = pl.program_id(1)

    @pl.when(k == 0)
    def _():
        acc_ref[...] = part

    @pl.when(jnp.logical_and(k > 0, k < nk - 1))
    def _():
        acc_ref[...] += part

    @pl.when(k == nk - 1)
    def _():
        finish(acc_ref[...] + part)


def matmul_residual_ln(a, w, res, g, b, tm, tk, name):
    m, kdim = a.shape
    n = w.shape[1]
    nk = kdim // tk
    return pl.pallas_call(
        functools.partial(_mm_ln_kernel, nk=nk),
        grid=(m // tm, nk),
        in_specs=[pl.BlockSpec((tm, tk), lambda i, k: (i, k)),
                  pl.BlockSpec((tk, n), lambda i, k: (k, 0)),
                  pl.BlockSpec((tm, n), lambda i, k: (i, 0)),
                  pl.BlockSpec((1, n), lambda i, k: (0, 0)),
                  pl.BlockSpec((1, n), lambda i, k: (0, 0))],
        out_specs=[pl.BlockSpec((tm, n), lambda i, k: (i, 0)),
                   pl.BlockSpec((tm, n), lambda i, k: (i, 0))],
        out_shape=[jax.ShapeDtypeStruct((m, n), F32), jax.ShapeDtypeStruct((m, n), BF16)],
        scratch_shapes=[pltpu.VMEM((tm, n), F32)],
        compiler_params=_params("arbitrary", "arbitrary"),
        name=name,
    )(a, w, res, g.reshape(1, n), b.reshape(1, n))


def _mlstm_kernel(qk_ref, v_ref, o_ref, gate_ref, cw_ref, cb_ref, gb_ref, ng_ref, out_ref,
                  qkbuf, c_state, n_state, m_state, *, rows):
    L = MLSTM_CHUNK
    H = MLSTM_HEADS
    D = HEAD_DIM
    W = MLSTM_WIDTH
    t = pl.program_id(1)

    @pl.when(t == 0)
    def _():
        qkbuf[0:SUBLANES, :] = jnp.zeros((SUBLANES, 2 * W), F32)
        c_state[...] = jnp.zeros_like(c_state)
        n_state[...] = jnp.zeros_like(n_state)
        m_state[...] = jnp.zeros_like(m_state)

    qkbuf[SUBLANES:SUBLANES + rows, :] = qk_ref[...]
    conv = cb_ref[...]
    for j in range(MLSTM_CONV_TAPS):
        off = SUBLANES - (MLSTM_CONV_TAPS - 1) + j
        conv = conv + cw_ref[j:j + 1, :] * qkbuf[off:off + rows, :]
    qkbuf[0:SUBLANES, :] = qkbuf[rows:rows + SUBLANES, :]
    qk = conv * _sigmoid(conv)

    gates = gate_ref[...] + gb_ref[...]
    logf = _log_sigmoid(gates)
    pos = lax.broadcasted_iota(jnp.int32, (rows, LANES), 0) % L
    bcum = logf
    shift = 1
    while shift < L:
        bcum = bcum + jnp.where(pos >= shift, pltpu.roll(bcum, shift, axis=0), 0.0)
        shift *= 2
    gates_t = gates.T
    bcum_t = bcum.T

    causal = (lax.broadcasted_iota(jnp.int32, (L, L), 0) >= lax.broadcasted_iota(jnp.int32, (L, L), 1))
    scale = D ** -0.5

    for h in range(H):
        c_prev = c_state[h]
        n_prev = n_state[h, 0:1, :]
        m_prev = m_state[h, 0:1, :]
        for c in range(rows // L):
            r0 = c * L
            q = qk[r0:r0 + L, h * D:(h + 1) * D]
            k = qk[r0:r0 + L, W + h * D:W + (h + 1) * D] * scale
            v = v_ref[r0:r0 + L, h * D:(h + 1) * D]
            ig_c = jnp.broadcast_to(gates[r0:r0 + L, h:h + 1], (L, LANES))
            b_c = jnp.broadcast_to(bcum[r0:r0 + L, H + h:H + h + 1], (L, LANES))
            ig_r = jnp.broadcast_to(gates_t[h:h + 1, r0:r0 + L], (L, L))
            b_r = jnp.broadcast_to(bcum_t[H + h:H + h + 1, r0:r0 + L], (L, L))
            g_tot = b_c[L - 1:L, :]

            q16 = q.astype(BF16)
            k16 = k.astype(BF16)
            v16 = v.astype(BF16)

            d_log = jnp.where(causal, b_c - b_r + ig_r, -jnp.inf)
            m_inter = b_c + m_prev
            m_t = jnp.maximum(jnp.max(d_log, axis=-1, keepdims=True), m_inter)
            s = lax.dot_general(q16, k16, (((1,), (1,)), ((), ())), preferred_element_type=F32)
            s = s * jnp.exp(d_log - m_t)
            w_inter = jnp.exp(m_inter - m_t)
            num = jnp.dot(s.astype(BF16), v16, preferred_element_type=F32)
            num = num + w_inter * jnp.dot(q16, c_prev.astype(BF16), preferred_element_type=F32)
            den = jnp.sum(s, axis=-1, keepdims=True) + w_inter * jnp.sum(q * n_prev, axis=-1, keepdims=True)
            hval = num / jnp.maximum(jnp.abs(den), jnp.exp(-m_t))

            mu = jnp.mean(hval, axis=-1, keepdims=True)
            dlt = hval - mu
            var = jnp.mean(dlt * dlt, axis=-1, keepdims=True)
            hn = dlt * lax.rsqrt(var + LN_EPS) * ng_ref[:, h * D:(h + 1) * D]
            og = _sigmoid(o_ref[r0:r0 + L, h * D:(h + 1) * D])
            out_ref[r0:r0 + L, h * D:(h + 1) * D] = (og * hn).astype(BF16)

            a_c = g_tot - b_c + ig_c
            a_max = jnp.max(a_c, axis=0, keepdims=True)
            ks = jnp.exp(a_c - a_max) * k
            kv = lax.dot_general(ks.astype(BF16), v16, (((0,), (0,)), ((), ())), preferred_element_type=F32)
            nk = jnp.sum(ks, axis=0, keepdims=True)
            m_new = jnp.maximum(g_tot + m_prev, a_max)
            dec = jnp.exp(g_tot + m_prev - m_new)
            wc = jnp.exp(a_max - m_new)
            c_prev = dec * c_prev + wc * kv
            n_prev = dec * n_prev + wc * nk
            m_prev = m_new
        c_state[h] = c_prev
        n_state[h] = jnp.broadcast_to(n_prev, (SUBLANES, D))
        m_state[h] = jnp.broadcast_to(m_prev, (SUBLANES, LANES))


def mlstm_heads(pm, gates, conv_w, conv_b, gate_b, norm_g, batch, seq, rows=256):
    m = pm.shape[0]
    w = MLSTM_WIDTH
    nt = seq // rows
    return pl.pallas_call(
        functools.partial(_mlstm_kernel, rows=rows),
        grid=(batch, nt),
        in_specs=[pl.BlockSpec((rows, 2 * w), lambda b, t: (b * nt + t, 0)),
                  pl.BlockSpec((rows, w), lambda b, t: (b * nt + t, 2)),
                  pl.BlockSpec((rows, w), lambda b, t: (b * nt + t, 3)),
                  pl.BlockSpec((rows, LANES), lambda b, t: (b * nt + t, 0)),
                  pl.BlockSpec((MLSTM_CONV_TAPS, 2 * w), lambda b, t: (0, 0)),
                  pl.BlockSpec((1, 2 * w), lambda b, t: (0, 0)),
                  pl.BlockSpec((1, LANES), lambda b, t: (0, 0)),
                  pl.BlockSpec((1, w), lambda b, t: (0, 0))],
        out_specs=pl.BlockSpec((rows, w), lambda b, t: (b * nt + t, 0)),
        out_shape=jax.ShapeDtypeStruct((m, w), BF16),
        scratch_shapes=[pltpu.VMEM((rows + SUBLANES, 2 * w), F32),
                        pltpu.VMEM((MLSTM_HEADS, HEAD_DIM, HEAD_DIM), F32),
                        pltpu.VMEM((MLSTM_HEADS, SUBLANES, HEAD_DIM), F32),
                        pltpu.VMEM((MLSTM_HEADS, SUBLANES, LANES), F32)],
        compiler_params=_params("arbitrary", "arbitrary"),
        name="mlstm",
    )(pm, pm, pm, gates, conv_w, conv_b.reshape(1, 2 * w), gate_b, norm_g.reshape(1, w))


CCONV_HALO = 32
CCONV_SUB = 32


def _cconv_kernel(a_ref, g_ref, cw_ref, cb_ref, lg_ref, lb_ref, out_ref, cbuf, *, rows):
    t = pl.program_id(1)

    @pl.when(t == 0)
    def _():
        cbuf[0:CCONV_HALO, :] = jnp.zeros((CCONV_HALO, CCONV_WIDTH), F32)

    cbuf[CCONV_HALO:CCONV_HALO + rows, :] = a_ref[...] * _sigmoid(g_ref[...])
    first = CCONV_HALO - (CCONV_TAPS - 1)
    for r0 in range(0, rows, CCONV_SUB):
        acc = jnp.broadcast_to(cb_ref[...], (CCONV_SUB, CCONV_WIDTH))
        for j in range(CCONV_TAPS):
            acc = acc + cw_ref[j:j + 1, :] * cbuf[first + j + r0:first + j + r0 + CCONV_SUB, :]
        y = _layer_norm_rows(acc, lg_ref[...], lb_ref[...])
        out_ref[r0:r0 + CCONV_SUB, :] = (y * _sigmoid(y)).astype(BF16)
    cbuf[0:CCONV_HALO, :] = cbuf[rows:rows + CCONV_HALO, :]


def conformer_conv(pc, conv_w, conv_b, ln_g, ln_b, batch, seq, rows=256):
    m = pc.shape[0]
    w = CCONV_WIDTH
    nt = seq // rows
    return pl.pallas_call(
        functools.partial(_cconv_kernel, rows=rows),
        grid=(batch, nt),
        in_specs=[pl.BlockSpec((rows, w), lambda b, t: (b * nt + t, 0)),
                  pl.BlockSpec((rows, w), lambda b, t: (b * nt + t, 1)),
                  pl.BlockSpec((CCONV_TAPS, w), lambda b, t: (0, 0)),
                  pl.BlockSpec((1, w), lambda b, t: (0, 0)),
                  pl.BlockSpec((1, w), lambda b, t: (0, 0)),
                  pl.BlockSpec((1, w), lambda b, t: (0, 0))],
        out_specs=pl.BlockSpec((rows, w), lambda b, t: (b * nt + t, 0)),
        out_shape=jax.ShapeDtypeStruct((m, w), BF16),
        scratch_shapes=[pltpu.VMEM((rows + CCONV_HALO, w), F32)],
        compiler_params=_params("arbitrary", "arbitrary"),
        name="cconv",
    )(pc, pc, conv_w, conv_b.reshape(1, w), ln_g.reshape(1, w), ln_b.reshape(1, w))


def _attn_bias_tiles(rel_bias):
    qb = ATTN_QBLOCK
    width = ATTN_BIAS_TILES * qb
    lq = jnp.arange(qb)[:, None]
    mm = jnp.arange(width)[None, :]
    dist = LEFT_CHUNKS * CHUNK + lq - mm
    cdiff = LEFT_CHUNKS + lq // CHUNK - mm // CHUNK
    valid = (cdiff >= 0) & (cdiff <= LEFT_CHUNKS)
    tab = rel_bias[:, jnp.clip(dist, -REL_CLIP, REL_CLIP) + REL_CLIP].astype(F32)
    tab = jnp.where(valid[None], tab, -jnp.inf)
    return tab.reshape(rel_bias.shape[0], qb, ATTN_BIAS_TILES, qb).transpose(0, 2, 1, 3)


def _chunk_attn_kernel(q_ref, k_ref, v_ref, bias_ref, out_ref):
    j = pl.program_id(1)
    qb = ATTN_QBLOCK
    lead = LEFT_CHUNKS * CHUNK // qb
    first_tile = jnp.maximum(j - lead, 0)
    bias_off = lead - jnp.minimum(j, lead)
    scale = HEAD_DIM ** -0.5
    for h in range(ATTN_HEADS):
        cols = slice(h * HEAD_DIM, (h + 1) * HEAD_DIM)
        q = q_ref[:, cols]
        scores = []
        for i in range(ATTN_KTILES):
            kt = k_ref[pl.ds(pl.multiple_of((first_tile + i) * qb, qb), qb), cols]
            s = lax.dot_general(q, kt, (((1,), (1,)), ((), ())), preferred_element_type=F32)
            scores.append(s * scale + bias_ref[h, bias_off + i])
        mx = scores[0].max(axis=-1, keepdims=True)
        for s in scores[1:]:
            mx = jnp.maximum(mx, s.max(axis=-1, keepdims=True))
        probs = [jnp.exp(s - mx) for s in scores]
        den = probs[0].sum(axis=-1, keepdims=True)
        for p in probs[1:]:
            den = den + p.sum(axis=-1, keepdims=True)
        inv = 1.0 / den
        acc = jnp.zeros((qb, HEAD_DIM), F32)
        for i in range(ATTN_KTILES):
            vt = v_ref[pl.ds(pl.multiple_of((first_tile + i) * qb, qb), qb), cols]
            acc = acc + jnp.dot((probs[i] * inv).astype(BF16), vt, preferred_element_type=F32)
        out_ref[:, cols] = acc.astype(BF16)


def chunk_attention(pa, bias_tiles, batch, seq):
    m = pa.shape[0]
    w = ATTN_WIDTH
    qb = ATTN_QBLOCK
    nq = seq // qb
    return pl.pallas_call(
        _chunk_attn_kernel,
        grid=(batch, nq),
        in_specs=[pl.BlockSpec((qb, w), lambda b, j: (b * nq + j, 0)),
                  pl.BlockSpec((seq, w), lambda b, j: (b, 1)),
                  pl.BlockSpec((seq, w), lambda b, j: (b, 2)),
                  pl.BlockSpec((ATTN_HEADS, ATTN_BIAS_TILES, qb, qb), lambda b, j: (0, 0, 0, 0))],
        out_specs=pl.BlockSpec((qb, w), lambda b, j: (b * nq + j, 0)),
        out_shape=jax.ShapeDtypeStruct((m, w), BF16),
        compiler_params=_params("arbitrary", "arbitrary"),
        name="chunk_attn",
    )(pa, pa, pa, bias_tiles)


def _cross_attn_kernel(q_ref, k_ref, v_ref, out_ref, *, head_dim):
    scale = head_dim ** -0.5
    for h in range(X_HEADS):
        cols = slice(h * head_dim, (h + 1) * head_dim)
        s = lax.dot_general(q_ref[:, cols], k_ref[:, cols], (((1,), (1,)), ((), ())),
                            preferred_element_type=F32) * scale
        mx = s.max(axis=-1, keepdims=True)
        p = jnp.exp(s - mx)
        p = p * (1.0 / p.sum(axis=-1, keepdims=True))
        out_ref[:, cols] = jnp.dot(p.astype(BF16), v_ref[:, cols], preferred_element_type=F32).astype(BF16)


def cross_attention(q, kv, batch, seq, n_mem, tq=512):
    m, d = q.shape
    nq = seq // tq
    return pl.pallas_call(
        functools.partial(_cross_attn_kernel, head_dim=d // X_HEADS),
        grid=(batch, nq),
        in_specs=[pl.BlockSpec((tq, d), lambda b, j: (b * nq + j, 0)),
                  pl.BlockSpec((n_mem, d), lambda b, j: (b, 0)),
                  pl.BlockSpec((n_mem, d), lambda b, j: (b, 1))],
        out_specs=pl.BlockSpec((tq, d), lambda b, j: (b * nq + j, 0)),
        out_shape=jax.ShapeDtypeStruct((m, d), BF16),
        compiler_params=_params("arbitrary", "arbitrary"),
        name="cross_attn",
    )(q, kv, kv)


def _ffn_up_kernel(x_ref, wg_ref, wv_ref, cw_ref, cb_ref, h_ref, gbuf, *, tm, tiles_per_seq):
    i = pl.program_id(1)
    x = x_ref[...]
    gate = jnp.dot(x, wg_ref[...], preferred_element_type=F32)
    val = jnp.dot(x, wv_ref[...], preferred_element_type=F32)

    @pl.when(i % tiles_per_seq == 0)
    def _():
        gbuf[0:SUBLANES, :] = jnp.zeros((SUBLANES, gbuf.shape[1]), F32)

    gbuf[SUBLANES:SUBLANES + tm, :] = gate
    conv = cb_ref[...] + cw_ref[FFN_CONV_TAPS - 1:FFN_CONV_TAPS, :] * gate
    for j in range(FFN_CONV_TAPS - 1):
        off = SUBLANES - (FFN_CONV_TAPS - 1) + j
        conv = conv + cw_ref[j:j + 1, :] * gbuf[off:off + tm, :]
    gbuf[0:SUBLANES, :] = gbuf[tm:tm + SUBLANES, :]
    gelu = 0.5 * conv * (1.0 + lax.erf(conv * (1.0 / math.sqrt(2.0))))
    h_ref[...] = (gelu * val).astype(BF16)


def ffn_up(x, w_up, conv_w, conv_b, seq, tm=1024, tn=512):
    m, d = x.shape
    dff = w_up.shape[1] // 2
    nn = dff // tn
    return pl.pallas_call(
        functools.partial(_ffn_up_kernel, tm=tm, tiles_per_seq=seq // tm),
        grid=(nn, m // tm),
        in_specs=[pl.BlockSpec((tm, d), lambda n, i: (i, 0)),
                  pl.BlockSpec((d, tn), lambda n, i: (0, n)),
                  pl.BlockSpec((d, tn), lambda n, i: (0, nn + n)),
                  pl.BlockSpec((FFN_CONV_TAPS, tn), lambda n, i: (0, n)),
                  pl.BlockSpec((1, tn), lambda n, i: (0, n))],
        out_specs=pl.BlockSpec((tm, tn), lambda n, i: (i, n)),
        out_shape=jax.ShapeDtypeStruct((m, dff), BF16),
        scratch_shapes=[pltpu.VMEM((tm + SUBLANES, tn), F32)],
        compiler_params=_params("arbitrary", "arbitrary"),
        name="ffn_up",
    )(x, w_up, w_up, conv_w, conv_b.reshape(1, dff))


def kernel(x, mem, ln_in_g, ln_in_b, w_in, mlstm_conv_w, mlstm_conv_b, mlstm_ig_b, mlstm_fg_b, mlstm_norm_g, cconv_w, cconv_b, cconv_ln_g, cconv_ln_b, rel_bias, w_out, ln1_g, ln1_b, xq, xkv, xo, ln2_g, ln2_b, ffn_w_up, ffn_conv_w, ffn_conv_b, ffn_w_down, ln3_g, ln3_b):
    batch, seq, d = x.shape
    n_mem = mem.shape[1]
    depth = w_in.shape[0]
    m = batch * seq
    w = MLSTM_WIDTH
    dff = ffn_w_down.shape[1]

    o_gate = 4 * w
    o_conv = o_gate + 2 * MLSTM_HEADS
    o_attn = o_conv + 2 * CCONV_WIDTH
    w_in16 = w_in.astype(BF16)
    w_m = w_in16[:, :, :o_gate]
    w_g = jnp.pad(w_in16[:, :, o_gate:o_conv], ((0, 0), (0, 0), (0, LANES - 2 * MLSTM_HEADS)))
    w_c = w_in16[:, :, o_conv:o_attn]
    w_a = w_in16[:, :, o_attn:]
    gate_b = jnp.pad(jnp.concatenate([mlstm_ig_b, mlstm_fg_b], axis=-1),
                     ((0, 0), (0, LANES - 2 * MLSTM_HEADS))).reshape(depth, 1, LANES)
    w_out16 = w_out.astype(BF16)
    xq16 = xq.astype(BF16)
    xkv16 = xkv.astype(BF16)
    xo16 = xo.astype(BF16)
    w_up16 = ffn_w_up.astype(BF16)
    w_down16 = ffn_w_down.astype(BF16)
    mem16 = mem.reshape(batch * n_mem, d).astype(BF16)

    xf, xb = layer_norm(x.reshape(m, d), ln_in_g, ln_in_b)
    for l in range(depth):
        pm = matmul(xb, w_m[l], F32, name="proj_mlstm")
        pg = matmul(xb, w_g[l], F32, name="proj_gates")
        pc = matmul(xb, w_c[l], F32, name="proj_cconv")
        pa = matmul(xb, w_a[l], BF16, name="proj_attn")
        h_m = mlstm_heads(pm, pg, mlstm_conv_w[l], mlstm_conv_b[l], gate_b[l], mlstm_norm_g[l], batch, seq)
        h_c = conformer_conv(pc, cconv_w[l], cconv_b[l], cconv_ln_g[l], cconv_ln_b[l], batch, seq)
        h_a = chunk_attention(pa, _attn_bias_tiles(rel_bias[l]), batch, seq)
        mix = jnp.concatenate([h_m, h_c, h_a], axis=-1)
        xf, xb = matmul_residual_ln(mix, w_out16[l], xf, ln1_g[l], ln1_b[l], tm=512, tk=d, name="mix_out_ln")
        q = matmul(xb, xq16[l], BF16, name="xattn_q")
        kv = matmul(mem16, xkv16[l], BF16, name="xattn_kv")
        ca = cross_attention(q, kv, batch, seq, n_mem)
        xf, xb = matmul_residual_ln(ca, xo16[l], xf, ln2_g[l], ln2_b[l], tm=512, tk=d, name="xattn_out_ln")
        hid = ffn_up(xb, w_up16[l], ffn_conv_w[l], ffn_conv_b[l], seq)
        xf, xb = matmul_residual_ln(hid, w_down16[l], xf, ln3_g[l], ln3_b[l], tm=512, tk=dff // 4, name="ffn_down_ln")
    return xf.reshape(batch, seq, d)
```

```python
import functools
import math

import numpy as np
import jax
import jax.numpy as jnp
from jax import lax
from jax.experimental import pallas as pl
from jax.experimental.pallas import tpu as pltpu

F32 = jnp.float32
BF16 = jnp.bfloat16

DEPTH = 4
CHUNK = 64
LEFT_CHUNKS = 8
REL_CLIP = 256
MLSTM_HEADS = 4
HEAD_DIM = 128
MLSTM_WIDTH = MLSTM_HEADS * HEAD_DIM
MLSTM_CONV_TAPS = 4
CCONV_WIDTH = 512
CCONV_TAPS = 31
ATTN_HEADS = 8
ATTN_WIDTH = ATTN_HEADS * HEAD_DIM
X_HEADS = 4
FFN_CONV_TAPS = 3
LN_EPS = 1e-5
DN_ALPHA = (2 * DEPTH) ** 0.25

LANES = 128
SUBLANES = 8
VMEM_LIMIT_BYTES = 56 * 1024 * 1024

MLSTM_CHUNK = 128
ATTN_QBLOCK = 4 * CHUNK
ATTN_LEAD = LEFT_CHUNKS * CHUNK // ATTN_QBLOCK
ATTN_KTILES = ATTN_LEAD + 1
ATTN_BIAS_TILES = ATTN_KTILES + ATTN_LEAD


def _params(*sem):
    return pltpu.CompilerParams(dimension_semantics=sem, vmem_limit_bytes=VMEM_LIMIT_BYTES)


def _sigmoid(x):
    return 1.0 / (1.0 + jnp.exp(-x))


def _log_sigmoid(x):
    return jnp.minimum(x, 0.0) - jnp.log1p(jnp.exp(-jnp.abs(x)))


def _layer_norm_rows(y, g, b):
    mu = jnp.mean(y, axis=-1, keepdims=True)
    d = y - mu
    var = jnp.mean(d * d, axis=-1, keepdims=True)
    return d * lax.rsqrt(var + LN_EPS) * g + b


def _ln_kernel(x_ref, g_ref, b_ref, of_ref, ob_ref):
    y = _layer_norm_rows(x_ref[...], g_ref[...], b_ref[...])
    of_ref[...] = y
    ob_ref[...] = y.astype(BF16)


def layer_norm(x, g, b, tm=512):
    m, d = x.shape
    return pl.pallas_call(
        _ln_kernel,
        grid=(m // tm,),
        in_specs=[pl.BlockSpec((tm, d), lambda i: (i, 0)),
                  pl.BlockSpec((1, d), lambda i: (0, 0)),
                  pl.BlockSpec((1, d), lambda i: (0, 0))],
        out_specs=[pl.BlockSpec((tm, d), lambda i: (i, 0)),
                   pl.BlockSpec((tm, d), lambda i: (i, 0))],
        out_shape=[jax.ShapeDtypeStruct((m, d), F32), jax.ShapeDtypeStruct((m, d), BF16)],
        compiler_params=_params("arbitrary"),
        name="ln_in",
    )(x, g.reshape(1, d), b.reshape(1, d))


def _mm_kernel(x_ref, w_ref, o_ref):
    o_ref[...] = jnp.dot(x_ref[...], w_ref[...], preferred_element_type=F32).astype(o_ref.dtype)


def matmul(x, w, out_dtype, tm=1024, tn=1024, name="mm"):
    m, k = x.shape
    n = w.shape[1]
    tm, tn = min(tm, m), min(tn, n)
    return pl.pallas_call(
        _mm_kernel,
        grid=(m // tm, n // tn),
        in_specs=[pl.BlockSpec((tm, k), lambda i, j: (i, 0)),
                  pl.BlockSpec((k, tn), lambda i, j: (0, j))],
        out_specs=pl.BlockSpec((tm, tn), lambda i, j: (i, j)),
        out_shape=jax.ShapeDtypeStruct((m, n), out_dtype),
        compiler_params=_params("arbitrary", "arbitrary"),
        name=name,
    )(x, w)


EPILOGUE_ROWS = 8
MM_COLUMN_GROUPS = 4


def _zero_after(v):
    bits = pltpu.bitcast(v, jnp.int32)
    return lax.shift_right_logical(lax.shift_right_logical(bits, 16), 16).astype(F32)


def _on_parity(step, body, buf_a, buf_b):
    @pl.when(step % 2 == 0)
    def _():
        body(buf_a, buf_b)

    @pl.when(step % 2 == 1)
    def _():
        body(buf_b, buf_a)


def _mm_ln_kernel(*refs, n_lhs, nk, tm):
    a_refs = refs[:n_lhs]
    w_ref, res_ref, g_ref, b_ref, of_ref, ob_ref, acc_a, acc_b = refs[n_lhs:]
    s = pl.program_id(0)
    k = pl.program_id(1)
    rq = tm // nk

    @pl.when(jnp.logical_and(s == 0, k == 0))
    def _():
        acc_a[...] = jnp.zeros_like(acc_a)
        acc_b[...] = jnp.zeros_like(acc_b)

    n = w_ref.shape[1]
    cw = n // MM_COLUMN_GROUPS
    rows_per_group = rq // MM_COLUMN_GROUPS

    def body(cur, prev):
        base = 0 if nk == 1 else pl.multiple_of(k * rq, rq)
        a = a_refs[0][...] if n_lhs == 1 else jnp.concatenate([r[...] for r in a_refs], axis=-1)
        for c in range(MM_COLUMN_GROUPS):
            for r in range(c * rows_per_group, (c + 1) * rows_per_group, EPILOGUE_ROWS):
                branch = prev[pl.ds(base + r, EPILOGUE_ROWS), :]
                y = _layer_norm_rows(DN_ALPHA * res_ref[r:r + EPILOGUE_ROWS, :] + branch, g_ref[...], b_ref[...])
                of_ref[r:r + EPILOGUE_ROWS, :] = y
                ob_ref[r:r + EPILOGUE_ROWS, :] = y.astype(BF16)
            cols = slice(c * cw, (c + 1) * cw)
            part = jnp.dot(a, w_ref[:, cols], preferred_element_type=F32) + _zero_after(y[0:1, 0:cw])
            if nk == 1:
                cur[:, cols] = part
            else:
                cur[:, cols] = jnp.where(k == 0, 0.0, cur[:, cols]) + part

    _on_parity(s, body, acc_a, acc_b)


def matmul_residual_ln(lhs, w, res, g, b, tm, nk, name):
    m = res.shape[0]
    kdim, n = w.shape
    nt = m // tm
    rq = tm // nk
    if len(lhs) > 1:
        assert nk == 1
        a_specs = [pl.BlockSpec((tm, a.shape[1]), lambda s, k: (jnp.minimum(s, nt - 1), 0)) for a in lhs]
    else:
        a_specs = [pl.BlockSpec((tm, kdim // nk), lambda s, k: (jnp.minimum(s, nt - 1), k))]
    prev_rows = lambda s, k: (jnp.where(s == 0, 0, (s - 1) * nk + k), 0)
    return pl.pallas_call(
        functools.partial(_mm_ln_kernel, n_lhs=len(lhs), nk=nk, tm=tm),
        grid=(nt + 1, nk),
        in_specs=a_specs + [pl.BlockSpec((kdim // nk, n), lambda s, k: (k, 0)),
                            pl.BlockSpec((rq, n), prev_rows),
                            pl.BlockSpec((1, n), lambda s, k: (0, 0)),
                            pl.BlockSpec((1, n), lambda s, k: (0, 0))],
        out_specs=[pl.BlockSpec((rq, n), prev_rows), pl.BlockSpec((rq, n), prev_rows)],
        out_shape=[jax.ShapeDtypeStruct((m, n), F32), jax.ShapeDtypeStruct((m, n), BF16)],
        scratch_shapes=[pltpu.VMEM((tm, n), F32), pltpu.VMEM((tm, n), F32)],
        compiler_params=_params("arbitrary", "arbitrary"),
        name=name,
    )(*lhs, w, res, g.reshape(1, n), b.reshape(1, n))


def _mlstm_kernel(qk_ref, v_ref, o_ref, gate_ref, cw_ref, cb_ref, gb_ref, ng_ref, out_ref,
                  qkbuf, c_state, n_state, m_state, *, rows):
    L = MLSTM_CHUNK
    H = MLSTM_HEADS
    D = HEAD_DIM
    W = MLSTM_WIDTH
    t = pl.program_id(1)

    @pl.when(t == 0)
    def _():
        qkbuf[0:SUBLANES, :] = jnp.zeros((SUBLANES, 2 * W), F32)
        c_state[...] = jnp.zeros_like(c_state)
        n_state[...] = jnp.zeros_like(n_state)
        m_state[...] = jnp.zeros_like(m_state)

    qkbuf[SUBLANES:SUBLANES + rows, :] = qk_ref[...]
    conv = cb_ref[...]
    for j in range(MLSTM_CONV_TAPS):
        off = SUBLANES - (MLSTM_CONV_TAPS - 1) + j
        conv = conv + cw_ref[j:j + 1, :] * qkbuf[off:off + rows, :]
    qkbuf[0:SUBLANES, :] = qkbuf[rows:rows + SUBLANES, :]
    qk = conv * _sigmoid(conv)

    gates = gate_ref[...] + gb_ref[...]
    logf = _log_sigmoid(gates)
    pos = lax.broadcasted_iota(jnp.int32, (rows, LANES), 0) % L
    bcum = logf
    shift = 1
    while shift < L:
        bcum = bcum + jnp.where(pos >= shift, pltpu.roll(bcum, shift, axis=0), 0.0)
        shift *= 2
    gates_t = gates.T
    bcum_t = bcum.T

    causal = (lax.broadcasted_iota(jnp.int32, (L, L), 0) >= lax.broadcasted_iota(jnp.int32, (L, L), 1))
    scale = D ** -0.5

    for h in range(H):
        c_prev = c_state[h]
        n_prev = n_state[h, 0:1, :]
        m_prev = m_state[h, 0:1, :]
        for c in range(rows // L):
            r0 = c * L
            q = qk[r0:r0 + L, h * D:(h + 1) * D]
            k = qk[r0:r0 + L, W + h * D:W + (h + 1) * D] * scale
            v = v_ref[r0:r0 + L, h * D:(h + 1) * D]
            ig_c = jnp.broadcast_to(gates[r0:r0 + L, h:h + 1], (L, LANES))
            b_c = jnp.broadcast_to(bcum[r0:r0 + L, H + h:H + h + 1], (L, LANES))
            ig_r = jnp.broadcast_to(gates_t[h:h + 1, r0:r0 + L], (L, L))
            b_r = jnp.broadcast_to(bcum_t[H + h:H + h + 1, r0:r0 + L], (L, L))
            g_tot = b_c[L - 1:L, :]

            q16 = q.astype(BF16)
            k16 = k.astype(BF16)
            v16 = v.astype(BF16)

            d_log = jnp.where(causal, b_c - b_r + ig_r, -jnp.inf)
            m_inter = b_c + m_prev
            m_t = jnp.maximum(jnp.max(d_log, axis=-1, keepdims=True), m_inter)
            s = lax.dot_general(q16, k16, (((1,), (1,)), ((), ())), preferred_element_type=F32)
            s = s * jnp.exp(d_log - m_t)
            w_inter = jnp.exp(m_inter - m_t)
            num = jnp.dot(s.astype(BF16), v16, preferred_element_type=F32)
            num = num + w_inter * jnp.dot(q16, c_prev.astype(BF16), preferred_element_type=F32)
            den = jnp.sum(s, axis=-1, keepdims=True) + w_inter * jnp.sum(q * n_prev, axis=-1, keepdims=True)
            hval = num / jnp.maximum(jnp.abs(den), jnp.exp(-m_t))

            mu = jnp.mean(hval, axis=-1, keepdims=True)
            dlt = hval - mu
            var = jnp.mean(dlt * dlt, axis=-1, keepdims=True)
            hn = dlt * lax.rsqrt(var + LN_EPS) * ng_ref[:, h * D:(h + 1) * D]
            og = _sigmoid(o_ref[r0:r0 + L, h * D:(h + 1) * D])
            out_ref[r0:r0 + L, h * D:(h + 1) * D] = (og * hn).astype(BF16)

            a_c = g_tot - b_c + ig_c
            a_max = jnp.max(a_c, axis=0, keepdims=True)
            ks = jnp.exp(a_c - a_max) * k
            kv = lax.dot_general(ks.astype(BF16), v16, (((0,), (0,)), ((), ())), preferred_element_type=F32)
            nk = jnp.sum(ks, axis=0, keepdims=True)
            m_new = jnp.maximum(g_tot + m_prev, a_max)
            dec = jnp.exp(g_tot + m_prev - m_new)
            wc = jnp.exp(a_max - m_new)
            c_prev = dec * c_prev + wc * kv
            n_prev = dec * n_prev + wc * nk
            m_prev = m_new
        c_state[h] = c_prev
        n_state[h] = jnp.broadcast_to(n_prev, (SUBLANES, D))
        m_state[h] = jnp.broadcast_to(m_prev, (SUBLANES, LANES))


def mlstm_heads(pm, gates, conv_w, conv_b, gate_b, norm_g, batch, seq, rows=256):
    m = pm.shape[0]
    w = MLSTM_WIDTH
    nt = seq // rows
    return pl.pallas_call(
        functools.partial(_mlstm_kernel, rows=rows),
        grid=(batch, nt),
        in_specs=[pl.BlockSpec((rows, 2 * w), lambda b, t: (b * nt + t, 0)),
                  pl.BlockSpec((rows, w), lambda b, t: (b * nt + t, 2)),
                  pl.BlockSpec((rows, w), lambda b, t: (b * nt + t, 3)),
                  pl.BlockSpec((rows, LANES), lambda b, t: (b * nt + t, 0)),
                  pl.BlockSpec((MLSTM_CONV_TAPS, 2 * w), lambda b, t: (0, 0)),
                  pl.BlockSpec((1, 2 * w), lambda b, t: (0, 0)),
                  pl.BlockSpec((1, LANES), lambda b, t: (0, 0)),
                  pl.BlockSpec((1, w), lambda b, t: (0, 0))],
        out_specs=pl.BlockSpec((rows, w), lambda b, t: (b * nt + t, 0)),
        out_shape=jax.ShapeDtypeStruct((m, w), BF16),
        scratch_shapes=[pltpu.VMEM((rows + SUBLANES, 2 * w), F32),
                        pltpu.VMEM((MLSTM_HEADS, HEAD_DIM, HEAD_DIM), F32),
                        pltpu.VMEM((MLSTM_HEADS, SUBLANES, HEAD_DIM), F32),
                        pltpu.VMEM((MLSTM_HEADS, SUBLANES, LANES), F32)],
        compiler_params=_params("arbitrary", "arbitrary"),
        name="mlstm",
    )(pm, pm, pm, gates, conv_w, conv_b.reshape(1, 2 * w), gate_b, norm_g.reshape(1, w))


CCONV_HALO = 32
CCONV_SUB = 32


def _cconv_kernel(a_ref, g_ref, cw_ref, cb_ref, lg_ref, lb_ref, out_ref, cbuf, shifted, *, rows):
    t = pl.program_id(1)

    @pl.when(t == 0)
    def _():
        cbuf[0:CCONV_HALO, :] = jnp.zeros((CCONV_HALO, CCONV_WIDTH), F32)

    cbuf[CCONV_HALO:CCONV_HALO + rows, :] = a_ref[...] * _sigmoid(g_ref[...])
    span = rows + CCONV_HALO - SUBLANES
    for m in range(1, SUBLANES):
        shifted[m, 0:span, :] = cbuf[SUBLANES - m:SUBLANES - m + span, :]
    for r0 in range(0, rows, CCONV_SUB):
        acc = jnp.broadcast_to(cb_ref[...], (CCONV_SUB, CCONV_WIDTH))
        for d in range(CCONV_TAPS):
            q, m = divmod(d, SUBLANES)
            u = r0 + CCONV_HALO - SUBLANES - SUBLANES * q
            src = cbuf[u + SUBLANES:u + SUBLANES + CCONV_SUB, :] if m == 0 else shifted[m, u:u + CCONV_SUB, :]
            acc = acc + cw_ref[CCONV_TAPS - 1 - d:CCONV_TAPS - d, :] * src
        y = _layer_norm_rows(acc, lg_ref[...], lb_ref[...])
        out_ref[r0:r0 + CCONV_SUB, :] = (y * _sigmoid(y)).astype(BF16)
    cbuf[0:CCONV_HALO, :] = cbuf[rows:rows + CCONV_HALO, :]


def conformer_conv(pc, conv_w, conv_b, ln_g, ln_b, batch, seq, rows=256):
    m = pc.shape[0]
    w = CCONV_WIDTH
    nt = seq // rows
    return pl.pallas_call(
        functools.partial(_cconv_kernel, rows=rows),
        grid=(batch, nt),
        in_specs=[pl.BlockSpec((rows, w), lambda b, t: (b * nt + t, 0)),
                  pl.BlockSpec((rows, w), lambda b, t: (b * nt + t, 1)),
                  pl.BlockSpec((CCONV_TAPS, w), lambda b, t: (0, 0)),
                  pl.BlockSpec((1, w), lambda b, t: (0, 0)),
                  pl.BlockSpec((1, w), lambda b, t: (0, 0)),
                  pl.BlockSpec((1, w), lambda b, t: (0, 0))],
        out_specs=pl.BlockSpec((rows, w), lambda b, t: (b * nt + t, 0)),
        out_shape=jax.ShapeDtypeStruct((m, w), BF16),
        scratch_shapes=[pltpu.VMEM((rows + CCONV_HALO, w), F32),
                        pltpu.VMEM((SUBLANES, rows + CCONV_HALO, w), F32)],
        compiler_params=_params("arbitrary", "arbitrary"),
        name="cconv",
    )(pc, pc, conv_w, conv_b.reshape(1, w), ln_g.reshape(1, w), ln_b.reshape(1, w))


def _attn_bias_tiles(rel_bias):
    qb = ATTN_QBLOCK
    width = ATTN_BIAS_TILES * qb
    period = width + qb
    x = np.arange(period)
    diff = np.where(x < width, -x, period - x)
    idx = np.clip(LEFT_CHUNKS * CHUNK + diff, -REL_CLIP, REL_CLIP) + REL_CLIP
    lead = rel_bias.shape[:-1]
    e = jnp.take(rel_bias.astype(F32), jnp.asarray(idx, jnp.int32), axis=-1)
    z = jnp.broadcast_to(e[..., None, :], lead + (qb, period)).reshape(lead + (qb * period,))
    z = z[..., :qb * (period - 1)].reshape(lead + (qb, period - 1))[..., :width]
    lq = np.arange(qb)[:, None]
    mm = np.arange(width)[None, :]
    cdiff = LEFT_CHUNKS + lq // CHUNK - mm // CHUNK
    valid = (cdiff >= 0) & (cdiff <= LEFT_CHUNKS)
    z = jnp.where(jnp.asarray(valid), z, -jnp.inf)
    return jnp.swapaxes(z.reshape(lead + (qb, ATTN_BIAS_TILES, qb)), -3, -2)


def _chunk_attn_kernel(q_ref, k_ref, v_ref, bias_ref, out_ref):
    j = pl.program_id(1)
    qb = ATTN_QBLOCK
    first_tile = jnp.maximum(j - ATTN_LEAD, 0)
    bias_off = ATTN_LEAD - jnp.minimum(j, ATTN_LEAD)
    scale = HEAD_DIM ** -0.5
    for h in range(ATTN_HEADS):
        cols = slice(h * HEAD_DIM, (h + 1) * HEAD_DIM)
        q = q_ref[:, cols]
        scores = []
        for i in range(ATTN_KTILES):
            kt = k_ref[pl.ds(pl.multiple_of((first_tile + i) * qb, qb), qb), cols]
            s = lax.dot_general(q, kt, (((1,), (1,)), ((), ())), preferred_element_type=F32)
            scores.append(s * scale + bias_ref[h, bias_off + i])
        top = scores[0]
        for s in scores[1:]:
            top = jnp.maximum(top, s)
        mx = top.max(axis=-1, keepdims=True)
        probs = [jnp.exp(s - mx) for s in scores]
        tot = probs[0]
        for p in probs[1:]:
            tot = tot + p
        den = tot.sum(axis=-1, keepdims=True)
        acc = None
        for i in range(ATTN_KTILES):
            vt = v_ref[pl.ds(pl.multiple_of((first_tile + i) * qb, qb), qb), cols]
            pv = jnp.dot(probs[i].astype(BF16), vt, preferred_element_type=F32)
            acc = pv if acc is None else acc + pv
        out_ref[:, cols] = (acc * (1.0 / den)).astype(BF16)


def chunk_attention(pa, bias_tiles, batch, seq):
    m = pa.shape[0]
    w = ATTN_WIDTH
    qb = ATTN_QBLOCK
    nq = seq // qb
    return pl.pallas_call(
        _chunk_attn_kernel,
        grid=(batch, nq),
        in_specs=[pl.BlockSpec((qb, w), lambda b, j: (b * nq + j, 0)),
                  pl.BlockSpec((seq, w), lambda b, j: (b, 1)),
                  pl.BlockSpec((seq, w), lambda b, j: (b, 2)),
                  pl.BlockSpec((ATTN_HEADS, ATTN_BIAS_TILES, qb, qb), lambda b, j: (0, 0, 0, 0))],
        out_specs=pl.BlockSpec((qb, w), lambda b, j: (b * nq + j, 0)),
        out_shape=jax.ShapeDtypeStruct((m, w), BF16),
        compiler_params=_params("arbitrary", "arbitrary"),
        name="chunk_attn",
    )(pa, pa, pa, bias_tiles)


def _cross_attn_kernel(q_ref, k_ref, v_ref, out_ref, *, head_dim):
    scale = head_dim ** -0.5
    for h in range(X_HEADS):
        cols = slice(h * head_dim, (h + 1) * head_dim)
        s = lax.dot_general(q_ref[:, cols], k_ref[:, cols], (((1,), (1,)), ((), ())),
                            preferred_element_type=F32) * scale
        mx = s.max(axis=-1, keepdims=True)
        p = jnp.exp(s - mx)
        p = p * (1.0 / p.sum(axis=-1, keepdims=True))
        out_ref[:, cols] = jnp.dot(p.astype(BF16), v_ref[:, cols], preferred_element_type=F32).astype(BF16)


def cross_attention(q, kv, batch, seq, n_mem, tq=512):
    m, d = q.shape
    nq = seq // tq
    return pl.pallas_call(
        functools.partial(_cross_attn_kernel, head_dim=d // X_HEADS),
        grid=(batch, nq),
        in_specs=[pl.BlockSpec((tq, d), lambda b, j: (b * nq + j, 0)),
                  pl.BlockSpec((n_mem, d), lambda b, j: (b, 0)),
                  pl.BlockSpec((n_mem, d), lambda b, j: (b, 1))],
        out_specs=pl.BlockSpec((tq, d), lambda b, j: (b * nq + j, 0)),
        out_shape=jax.ShapeDtypeStruct((m, d), BF16),
        compiler_params=_params("arbitrary", "arbitrary"),
        name="cross_attn",
    )(q, kv, kv)


def _ffn_up_kernel(x_ref, wg_ref, wv_ref, cw_ref, cb_ref, h_ref, gbuf, *, tm, tiles_per_seq):
    i = pl.program_id(1)
    x = x_ref[...]
    gate = jnp.dot(x, wg_ref[...], preferred_element_type=F32)
    val = jnp.dot(x, wv_ref[...], preferred_element_type=F32)

    @pl.when(i % tiles_per_seq == 0)
    def _():
        gbuf[0:SUBLANES, :] = jnp.zeros((SUBLANES, gbuf.shape[1]), F32)

    gbuf[SUBLANES:SUBLANES + tm, :] = gate
    conv = cb_ref[...] + cw_ref[FFN_CONV_TAPS - 1:FFN_CONV_TAPS, :] * gate
    for j in range(FFN_CONV_TAPS - 1):
        off = SUBLANES - (FFN_CONV_TAPS - 1) + j
        conv = conv + cw_ref[j:j + 1, :] * gbuf[off:off + tm, :]
    gbuf[0:SUBLANES, :] = gbuf[tm:tm + SUBLANES, :]
    gelu = 0.5 * conv * (1.0 + lax.erf(conv * (1.0 / math.sqrt(2.0))))
    h_ref[...] = (gelu * val).astype(BF16)


def ffn_up(x, w_up, conv_w, conv_b, seq, tm=1024, tn=512):
    m, d = x.shape
    dff = w_up.shape[1] // 2
    nn = dff // tn
    tm = min(tm, seq)
    return pl.pallas_call(
        functools.partial(_ffn_up_kernel, tm=tm, tiles_per_seq=seq // tm),
        grid=(nn, m // tm),
        in_specs=[pl.BlockSpec((tm, d), lambda n, i: (i, 0)),
                  pl.BlockSpec((d, tn), lambda n, i: (0, n)),
                  pl.BlockSpec((d, tn), lambda n, i: (0, nn + n)),
                  pl.BlockSpec((FFN_CONV_TAPS, tn), lambda n, i: (0, n)),
                  pl.BlockSpec((1, tn), lambda n, i: (0, n))],
        out_specs=pl.BlockSpec((tm, tn), lambda n, i: (i, n)),
        out_shape=jax.ShapeDtypeStruct((m, dff), BF16),
        scratch_shapes=[pltpu.VMEM((tm + SUBLANES, tn), F32)],
        compiler_params=_params("arbitrary", "arbitrary"),
        name="ffn_up",
    )(x, w_up, w_up, conv_w, conv_b.reshape(1, dff))


def kernel(x, mem, ln_in_g, ln_in_b, w_in, mlstm_conv_w, mlstm_conv_b, mlstm_ig_b, mlstm_fg_b, mlstm_norm_g, cconv_w, cconv_b, cconv_ln_g, cconv_ln_b, rel_bias, w_out, ln1_g, ln1_b, xq, xkv, xo, ln2_g, ln2_b, ffn_w_up, ffn_conv_w, ffn_conv_b, ffn_w_down, ln3_g, ln3_b):
    batch, seq, d = x.shape
    n_mem = mem.shape[1]
    depth = w_in.shape[0]
    m = batch * seq
    w = MLSTM_WIDTH
    dff = ffn_w_down.shape[1]

    o_gate = 4 * w
    o_conv = o_gate + 2 * MLSTM_HEADS
    o_attn = o_conv + 2 * CCONV_WIDTH
    w_in16 = w_in.astype(BF16)
    w_m = w_in16[:, :, :o_gate]
    w_g = jnp.pad(w_in16[:, :, o_gate:o_conv], ((0, 0), (0, 0), (0, LANES - 2 * MLSTM_HEADS)))
    w_c = w_in16[:, :, o_conv:o_attn]
    w_a = w_in16[:, :, o_attn:]
    gate_b = jnp.pad(jnp.concatenate([mlstm_ig_b, mlstm_fg_b], axis=-1),
                     ((0, 0), (0, LANES - 2 * MLSTM_HEADS))).reshape(depth, 1, LANES)
    w_out16 = w_out.astype(BF16)
    xq16 = xq.astype(BF16)
    xkv16 = xkv.astype(BF16)
    xo16 = xo.astype(BF16)
    w_up16 = ffn_w_up.astype(BF16)
    w_down16 = ffn_w_down.astype(BF16)
    mem16 = mem.reshape(batch * n_mem, d).astype(BF16)
    bias_tiles = _attn_bias_tiles(rel_bias)

    xf, xb = layer_norm(x.reshape(m, d), ln_in_g, ln_in_b)
    for l in range(depth):
        pm = matmul(xb, w_m[l], F32, name="proj_mlstm")
        pg = matmul(xb, w_g[l], F32, name="proj_gates")
        pc = matmul(xb, w_c[l], F32, name="proj_cconv")
        pa = matmul(xb, w_a[l], BF16, name="proj_attn")
        h_m = mlstm_heads(pm, pg, mlstm_conv_w[l], mlstm_conv_b[l], gate_b[l], mlstm_norm_g[l], batch, seq)
        h_c = conformer_conv(pc, cconv_w[l], cconv_b[l], cconv_ln_g[l], cconv_ln_b[l], batch, seq)
        h_a = chunk_attention(pa, bias_tiles[l], batch, seq)
        xf, xb = matmul_residual_ln([h_m, h_c, h_a], w_out16[l], xf, ln1_g[l], ln1_b[l],
                                    tm=512, nk=1, name="mix_out_ln")
        q = matmul(xb, xq16[l], BF16, name="xattn_q")
        kv = matmul(mem16, xkv16[l], BF16, name="xattn_kv")
        ca = cross_attention(q, kv, batch, seq, n_mem)
        xf, xb = matmul_residual_ln([ca], xo16[l], xf, ln2_g[l], ln2_b[l], tm=512, nk=1, name="xattn_out_ln")
        hid = ffn_up(xb, w_up16[l], ffn_conv_w[l], ffn_conv_b[l], seq)
        xf, xb = matmul_residual_ln([hid], w_down16[l], xf, ln3_g[l], ln3_b[l], tm=1024, nk=4, name="ffn_down_ln")
    return xf.reshape(batch, seq, d)
```
